```python
import jax, jax.numpy as jnp
from jax import lax
import numpy as np

D_MODEL = 2048
BATCH = 4
SEQ = 4096
DEPTH = 1
DEC_BATCH = 32
DEC_SEQ = 64
PAST_LEN = 1024

CHUNK = 64
LEFT_CHUNKS = 8
LEFT_FRAMES = LEFT_CHUNKS * CHUNK
BAND = LEFT_FRAMES + CHUNK
N_HEADS = 16
HEAD_DIM = 64
D_ATTN = N_HEADS * HEAD_DIM
MAX_REL = 128
N_REL = 2 * MAX_REL + 1
D_CONV = 1024
CONV_WIDTH = 31
N_GROUPS = 4
EXPERTS_PER_GROUP = 4
N_EXPERTS = N_GROUPS * EXPERTS_PER_GROUP
TOP_K = 2
D_EXPERT = 512
ALPHA = (2.0 * DEPTH) ** 0.25
BETA = (8.0 * DEPTH) ** -0.25
LN_EPS = 1e-5
D_IN = 3 * D_ATTN + 2 * D_CONV + 2 * D_MODEL
SPLITS = (D_ATTN, 2 * D_ATTN, 3 * D_ATTN, 3 * D_ATTN + D_CONV, 3 * D_ATTN + 2 * D_CONV,
          3 * D_ATTN + 2 * D_CONV + D_MODEL)

kernel_name = 'streaming_conformer_hybrid_moe_step'


def layer_norm(x, g, b):
    xf = x.astype(jnp.float32)
    mu = jnp.mean(xf, axis=-1, keepdims=True)
    var = jnp.mean(jnp.square(xf - mu), axis=-1, keepdims=True)
    return ((xf - mu) * lax.rsqrt(var + LN_EPS) * g.astype(jnp.float32) + b.astype(jnp.float32)).astype(x.dtype)


def rel_bias(rel_table, dist):
    idx = jnp.clip(dist, -MAX_REL, MAX_REL) + MAX_REL
    return rel_table[:, idx]


def attend(q, k, v, bias, mask):
    s = jnp.einsum('bqhd,bkhd->bhqk', q, k).astype(jnp.float32) * (HEAD_DIM ** -0.5)
    s = s + bias.astype(jnp.float32)
    if mask is not None:
        s = jnp.where(mask, s, jnp.finfo(jnp.float32).min)
    p = jax.nn.softmax(s, axis=-1).astype(v.dtype)
    return jnp.einsum('bhqk,bkhd->bqhd', p, v)


def band_attention_prompt(q, k, v, rel_table):
    B, S, H, Dh = q.shape
    nc = S // CHUNK
    pad = ((0, 0), (LEFT_FRAMES, 0), (0, 0), (0, 0))
    kp = jnp.pad(k, pad)
    vp = jnp.pad(v, pad)
    qi = jnp.arange(CHUNK)[:, None]
    kj = jnp.arange(BAND)[None, :]
    bias = rel_bias(rel_table, qi + LEFT_FRAMES - kj)
    qc = q.reshape(B, nc, CHUNK, H, Dh).transpose(1, 0, 2, 3, 4)

    def one_chunk(args):
        c, q_c = args
        start = c * CHUNK
        k_band = lax.dynamic_slice_in_dim(kp, start, BAND, axis=1)
        v_band = lax.dynamic_slice_in_dim(vp, start, BAND, axis=1)
        valid = kj >= LEFT_FRAMES - start
        return attend(q_c, k_band, v_band, bias, valid)

    out = lax.map(one_chunk, (jnp.arange(nc), qc))
    return out.transpose(1, 0, 2, 3, 4).reshape(B, S, H, Dh)


def band_attention_sample(q, k_all, v_all, rel_table, n_past):
    T = q.shape[1]
    qi = jnp.arange(T)[:, None]
    kj = jnp.arange(n_past + T)[None, :]
    bias = rel_bias(rel_table, qi + n_past - kj)
    return attend(q, k_all, v_all, bias, None)


def causal_depthwise_conv(u_pad, dw_w, dw_b):
    out = lax.conv_general_dilated(u_pad, dw_w[:, None, :], window_strides=(1,), padding='VALID',
                                   dimension_numbers=('NWC', 'WIO', 'NWC'),
                                   feature_group_count=D_CONV)
    return out + dw_b


def hier_moe(x, w_rg, b_rg, w_re, b_re, w_gate, w_up, w_down):
    B, T, D = x.shape
    xt = x.reshape(B * T, D)
    gl = (xt @ w_rg + b_rg).astype(jnp.float32)
    pg = jax.nn.softmax(gl, axis=-1)
    _, g_sel = lax.top_k(gl, 1)
    p_g = jnp.take_along_axis(pg, g_sel, axis=-1)
    el = (jnp.einsum('nd,dge->nge', xt, w_re) + b_re).astype(jnp.float32)
    el_sel = jnp.take_along_axis(el, g_sel[:, :, None], axis=1)[:, 0]
    top_logit, top_idx = lax.top_k(el_sel, TOP_K)
    w = p_g * jax.nn.softmax(top_logit, axis=-1)
    expert_id = g_sel * EXPERTS_PER_GROUP + top_idx
    combine = jnp.sum(jax.nn.one_hot(expert_id, N_EXPERTS, dtype=jnp.float32) * w[..., None], axis=1)
    combine = combine.astype(x.dtype)
    y = jnp.zeros_like(xt)
    for e in range(N_EXPERTS):
        h = jax.nn.silu(xt @ w_gate[e]) * (xt @ w_up[e])
        y = y + combine[:, e:e + 1] * (h @ w_down[e])
    return y.reshape(B, T, D)


def encoder_layer(x, k_cache, v_cache, conv_cache, w_in, b_in, rel_table, dw_w, dw_b, conv_ln_g,
                  conv_ln_b, w_attn_out, w_conv_out, w_o, ln1_g, ln1_b, w_rg, b_rg, w_re, b_re,
                  w_gate, w_up, w_down, ln2_g, ln2_b):
    B, T, _ = x.shape
    z = x @ w_in + b_in
    q, k, v, u_a, u_b, g_a, g_c = jnp.split(z, SPLITS, axis=-1)
    q = q.reshape(B, T, N_HEADS, HEAD_DIM)
    k = k.reshape(B, T, N_HEADS, HEAD_DIM)
    v = v.reshape(B, T, N_HEADS, HEAD_DIM)
    u = u_a * jax.nn.sigmoid(u_b)
    if k_cache is None:
        attn = band_attention_prompt(q, k, v, rel_table)
        n_keep = min(LEFT_FRAMES, T)
        k_state = k[:, T - n_keep:]
        v_state = v[:, T - n_keep:]
        u_pad = jnp.pad(u, ((0, 0), (CONV_WIDTH - 1, 0), (0, 0)))
    else:
        n_past = k_cache.shape[1]
        k_all = jnp.concatenate([k_cache, k], axis=1)
        v_all = jnp.concatenate([v_cache, v], axis=1)
        attn = band_attention_sample(q, k_all, v_all, rel_table, n_past)
        k_state = k_all[:, k_all.shape[1] - n_past:]
        v_state = v_all[:, v_all.shape[1] - n_past:]
        u_pad = jnp.concatenate([conv_cache, u], axis=1)
    conv_state = u_pad[:, u_pad.shape[1] - (CONV_WIDTH - 1):]
    c = jax.nn.silu(layer_norm(causal_depthwise_conv(u_pad, dw_w, dw_b), conv_ln_g, conv_ln_b))
    a_branch = attn.reshape(B, T, D_ATTN) @ w_attn_out
    c_branch = c @ w_conv_out
    m = (jax.nn.sigmoid(g_a) * a_branch + jax.nn.sigmoid(g_c) * c_branch) @ w_o
    h = layer_norm(ALPHA * x + m, ln1_g, ln1_b)
    y = layer_norm(ALPHA * h + hier_moe(h, w_rg, b_rg, w_re, b_re, w_gate, w_up, w_down), ln2_g, ln2_b)
    return y, k_state, v_state, conv_state


def setup_inputs(seed: int = 0) -> dict:
    key = jax.random.key(seed)
    ks = jax.random.split(key, 26)
    f32 = jnp.float32
    n_cache = min(LEFT_FRAMES, PAST_LEN)

    def nrm(k, shape, scale):
        return jax.random.normal(k, shape, f32) * scale

    col_scale = jnp.concatenate([jnp.ones((2 * D_ATTN,), f32), jnp.full((D_ATTN,), BETA, f32),
                                 jnp.ones((2 * D_CONV + 2 * D_MODEL,), f32)])
    return {
        'x_prompt': nrm(ks[0], (BATCH, SEQ, D_MODEL), 1.0),
        'x_sample': nrm(ks[1], (DEC_BATCH, DEC_SEQ, D_MODEL), 1.0),
        'cache_attn_k': nrm(ks[2], (DEPTH, DEC_BATCH, n_cache, N_HEADS, HEAD_DIM), 1.0),
        'cache_attn_v': nrm(ks[3], (DEPTH, DEC_BATCH, n_cache, N_HEADS, HEAD_DIM), BETA),
        'cache_conv': nrm(ks[4], (DEPTH, DEC_BATCH, CONV_WIDTH - 1, D_CONV), 0.5),
        'w_in': nrm(ks[5], (DEPTH, D_MODEL, D_IN), D_MODEL ** -0.5) * col_scale,
        'b_in': nrm(ks[6], (DEPTH, D_IN), 0.02),
        'rel_table': nrm(ks[7], (DEPTH, N_HEADS, N_REL), 0.5),
        'dw_w': nrm(ks[8], (DEPTH, CONV_WIDTH, D_CONV), CONV_WIDTH ** -0.5),
        'dw_b': nrm(ks[9], (DEPTH, D_CONV), 0.02),
        'conv_ln_g': 1.0 + nrm(ks[10], (DEPTH, D_CONV), 0.02),
        'conv_ln_b': nrm(ks[11], (DEPTH, D_CONV), 0.02),
        'w_attn_out': nrm(ks[12], (DEPTH, D_ATTN, D_MODEL), D_ATTN ** -0.5),
        'w_conv_out': nrm(ks[13], (DEPTH, D_CONV, D_MODEL), D_CONV ** -0.5),
        'w_o': nrm(ks[14], (DEPTH, D_MODEL, D_MODEL), BETA * D_MODEL ** -0.5),
        'ln1_g': 1.0 + nrm(ks[15], (DEPTH, D_MODEL), 0.02),
        'ln1_b': nrm(ks[16], (DEPTH, D_MODEL), 0.02),
        'w_router_group': nrm(ks[17], (DEPTH, D_MODEL, N_GROUPS), D_MODEL ** -0.5),
        'b_router_group': nrm(ks[18], (DEPTH, N_GROUPS), 0.01),
        'w_router_expert': nrm(ks[19], (DEPTH, D_MODEL, N_GROUPS, EXPERTS_PER_GROUP), D_MODEL ** -0.5),
        'b_router_expert': nrm(ks[20], (DEPTH, N_GROUPS, EXPERTS_PER_GROUP), 0.01),
        'w_gate': nrm(ks[21], (DEPTH, N_EXPERTS, D_MODEL, D_EXPERT), BETA * D_MODEL ** -0.5),
        'w_up': nrm(ks[22], (DEPTH, N_EXPERTS, D_MODEL, D_EXPERT), BETA * D_MODEL ** -0.5),
        'w_down': nrm(ks[23], (DEPTH, N_EXPERTS, D_EXPERT, D_MODEL), BETA * D_EXPERT ** -0.5),
        'ln2_g': 1.0 + nrm(ks[24], (DEPTH, D_MODEL), 0.02),
        'ln2_b': nrm(ks[25], (DEPTH, D_MODEL), 0.02),
    }


def reference(x_prompt, x_sample, cache_attn_k, cache_attn_v, cache_conv, w_in, b_in, rel_table,
              dw_w, dw_b, conv_ln_g, conv_ln_b, w_attn_out, w_conv_out, w_o, ln1_g, ln1_b,
              w_router_group, b_router_group, w_router_expert, b_router_expert, w_gate, w_up,
              w_down, ln2_g, ln2_b):
    yp, ys = x_prompt, x_sample
    kp_l, vp_l, cp_l, ks_l, vs_l, cs_l = [], [], [], [], [], []
    for l in range(DEPTH):
        params = (w_in[l], b_in[l], rel_table[l], dw_w[l], dw_b[l], conv_ln_g[l], conv_ln_b[l],
                  w_attn_out[l], w_conv_out[l], w_o[l], ln1_g[l], ln1_b[l], w_router_group[l],
                  b_router_group[l], w_router_expert[l], b_router_expert[l], w_gate[l], w_up[l],
                  w_down[l], ln2_g[l], ln2_b[l])
        yp, kp, vp, cp = encoder_layer(yp, None, None, None, *params)
        ys, kn, vn, cn = encoder_layer(ys, cache_attn_k[l], cache_attn_v[l], cache_conv[l], *params)
        kp_l.append(kp)
        vp_l.append(vp)
        cp_l.append(cp)
        ks_l.append(kn)
        vs_l.append(vn)
        cs_l.append(cn)
    return (yp, ys, jnp.stack(kp_l), jnp.stack(vp_l), jnp.stack(cp_l),
            jnp.stack(ks_l), jnp.stack(vs_l), jnp.stack(cs_l))
```

```python
import functools

import jax
import jax.numpy as jnp
from jax import lax
from jax.experimental import pallas as pl
from jax.experimental.pallas import tpu as pltpu

F32 = jnp.float32
BF16 = jnp.bfloat16

D_MODEL = 2048
N_HEADS = 16
HEAD_DIM = 64
D_ATTN = N_HEADS * HEAD_DIM
CHUNK = 64
LEFT_FRAMES = 512
BAND = LEFT_FRAMES + CHUNK
MAX_REL = 128
D_CONV = 1024
CONV_WIDTH = 31
N_GROUPS = 4
EXPERTS_PER_GROUP = 4
N_EXPERTS = N_GROUPS * EXPERTS_PER_GROUP
D_EXPERT = 512
LN_EPS = 1e-5

LANES = 128
HEAD_PAIRS = D_ATTN // LANES
CONV_HALO = 32
MOE_TILE = 256
VMEM_LIMIT = 56 * 1024 * 1024


def _params(sem, vmem=VMEM_LIMIT):
    return pltpu.CompilerParams(dimension_semantics=sem, vmem_limit_bytes=vmem)


def _pick_tile(n, prefs):
    for t in prefs:
        if n % t == 0:
            return t
    raise ValueError(f"no tile in {prefs} divides {n}")


def _layer_norm(x, g, b):
    mu = jnp.mean(x, axis=-1, keepdims=True)
    xc = x - mu
    var = jnp.mean(xc * xc, axis=-1, keepdims=True)
    return xc * lax.rsqrt(var + LN_EPS) * g + b


def _inproj_kernel(x_ref, w_ref, b_ref, qkv_ref, u_ref, g_ref, xb_ref):
    j = pl.program_id(1)

    @pl.when(j == 0)
    def _():
        xb_ref[...] = x_ref[...].astype(BF16)

    z = jnp.dot(xb_ref[...], w_ref[...], preferred_element_type=F32) + b_ref[...]

    @pl.when(j < 3)
    def _():
        qkv_ref[...] = z

    @pl.when(jnp.logical_and(j >= 3, j < 5))
    def _():
        half = z.shape[1] // 2
        u_ref[...] = z[:, :half] * jax.nn.sigmoid(z[:, half:])

    @pl.when(j >= 5)
    def _():
        g_ref[...] = jax.nn.sigmoid(z).astype(BF16)


def _in_projection(x, w_perm, b_perm):
    n = x.shape[0]
    tm = _pick_tile(n, (512, 256, 128, 64))
    tn = 1024
    nj = w_perm.shape[1] // tn
    return pl.pallas_call(
        _inproj_kernel,
        grid=(n // tm, nj),
        in_specs=[
            pl.BlockSpec((tm, D_MODEL), lambda i, j: (i, 0)),
            pl.BlockSpec((D_MODEL, tn), lambda i, j: (0, j)),
            pl.BlockSpec((1, tn), lambda i, j: (0, j)),
        ],
        out_specs=[
            pl.BlockSpec((tm, tn), lambda i, j: (i, jnp.minimum(j, 2))),
            pl.BlockSpec((tm, tn // 2), lambda i, j: (i, jnp.clip(j - 3, 0, 1))),
            pl.BlockSpec((tm, tn), lambda i, j: (i, jnp.clip(j - 5, 0, 3))),
        ],
        out_shape=[
            jax.ShapeDtypeStruct((n, 3 * D_ATTN), F32),
            jax.ShapeDtypeStruct((n, D_CONV), F32),
            jax.ShapeDtypeStruct((n, 2 * D_MODEL), BF16),
        ],
        scratch_shapes=[pltpu.VMEM((tm, D_MODEL), BF16)],
        compiler_params=_params(("arbitrary", "arbitrary")),
        name="in_projection",
    )(x, w_perm, b_perm)


def _attn_body(q_ref, kp_ref, kc_ref, vp_ref, vc_ref, bias_ref, o_ref, kk_ref, vv_ref, *, n_chunks, first_pos):
    cur = n_chunks * CHUNK
    kk_ref[0:LEFT_FRAMES, :] = kp_ref[...].astype(BF16)
    kk_ref[LEFT_FRAMES:LEFT_FRAMES + cur, :] = kc_ref[...].astype(BF16)
    vv_ref[0:LEFT_FRAMES, :] = vp_ref[...].astype(BF16)
    vv_ref[LEFT_FRAMES:LEFT_FRAMES + cur, :] = vc_ref[...].astype(BF16)

    lane = lax.broadcasted_iota(jnp.int32, (CHUNK, LANES), 1)
    low = lane < HEAD_DIM
    key_idx = lax.broadcasted_iota(jnp.int32, (1, BAND), 1)
    neg = jnp.finfo(F32).min

    def chunk(c, carry):
        r0 = pl.multiple_of(c * CHUNK, CHUNK)
        for p in range(HEAD_PAIRS):
            cols = slice(p * LANES, (p + 1) * LANES)
            q2 = (q_ref[pl.ds(r0, CHUNK), cols] * (HEAD_DIM ** -0.5)).astype(BF16)
            zero = jnp.zeros_like(q2)
            qq = jnp.concatenate([jnp.where(low, q2, zero), jnp.where(low, zero, q2)], axis=0)
            kb = kk_ref[pl.ds(r0, BAND), cols]
            s = lax.dot_general(qq, kb, (((1,), (1,)), ((), ())), preferred_element_type=F32)
            s = s + bias_ref[p]
            if first_pos is not None:
                s = jnp.where(key_idx + (first_pos + r0) >= 0, s, neg)
            m = jnp.max(s, axis=-1, keepdims=True)
            e = jnp.exp(s - m)
            l = jnp.sum(e, axis=-1, keepdims=True)
            vb = vv_ref[pl.ds(r0, BAND), cols]
            o2 = jnp.dot(e.astype(BF16), vb, preferred_element_type=F32) / l
            o = jnp.where(low, o2[:CHUNK], o2[CHUNK:])
            o_ref[pl.ds(r0, CHUNK), cols] = o.astype(o_ref.dtype)
        return carry

    lax.fori_loop(0, n_chunks, chunk, 0)


def _attn_prompt_kernel(q_ref, kp_ref, kc_ref, vp_ref, vc_ref, bias_ref, o_ref, kk_ref, vv_ref, *, n_chunks):
    first_pos = (pl.program_id(1) - 1) * LEFT_FRAMES
    _attn_body(q_ref, kp_ref, kc_ref, vp_ref, vc_ref, bias_ref, o_ref, kk_ref, vv_ref,
               n_chunks=n_chunks, first_pos=first_pos)


def _attn_sample_kernel(q_ref, kp_ref, kc_ref, vp_ref, vc_ref, bias_ref, o_ref, ks_ref, vs_ref, kk_ref, vv_ref):
    _attn_body(q_ref, kp_ref, kc_ref, vp_ref, vc_ref, bias_ref, o_ref, kk_ref, vv_ref,
               n_chunks=1, first_pos=None)
    keep = LEFT_FRAMES - CHUNK
    ks_ref[0:keep, :] = kp_ref[CHUNK:, :]
    ks_ref[keep:, :] = kc_ref[...]
    vs_ref[0:keep, :] = vp_ref[CHUNK:, :]
    vs_ref[keep:, :] = vc_ref[...]


def _attention_prompt(qkv, bias2, batch, seq):
    blk = LEFT_FRAMES
    nsb = seq // blk
    n_p = batch * seq
    row = lambda b, i: b * nsb + i
    prev = lambda b, i: b * nsb + jnp.maximum(i - 1, 0)
    return pl.pallas_call(
        functools.partial(_attn_prompt_kernel, n_chunks=blk // CHUNK),
        grid=(batch, nsb),
        in_specs=[
            pl.BlockSpec((blk, D_ATTN), lambda b, i: (row(b, i), 0)),
            pl.BlockSpec((blk, D_ATTN), lambda b, i: (prev(b, i), 1)),
            pl.BlockSpec((blk, D_ATTN), lambda b, i: (row(b, i), 1)),
            pl.BlockSpec((blk, D_ATTN), lambda b, i: (prev(b, i), 2)),
            pl.BlockSpec((blk, D_ATTN), lambda b, i: (row(b, i), 2)),
            pl.BlockSpec((HEAD_PAIRS, 2 * CHUNK, BAND), lambda b, i: (0, 0, 0)),
        ],
        out_specs=pl.BlockSpec((blk, D_ATTN), lambda b, i: (row(b, i), 0)),
        out_shape=jax.ShapeDtypeStruct((n_p, D_ATTN), BF16),
        scratch_shapes=[pltpu.VMEM((2 * blk, D_ATTN), BF16), pltpu.VMEM((2 * blk, D_ATTN), BF16)],
        compiler_params=_params(("arbitrary", "arbitrary")),
        name="attention_prompt",
    )(qkv, qkv, qkv, qkv, qkv, bias2)


def _attention_sample(qkv, cache_k, cache_v, bias2, row0, dec_batch):
    blk0 = row0 // CHUNK
    state = jax.ShapeDtypeStruct((dec_batch, LEFT_FRAMES, D_ATTN), F32)
    return pl.pallas_call(
        _attn_sample_kernel,
        grid=(dec_batch,),
        in_specs=[
            pl.BlockSpec((CHUNK, D_ATTN), lambda b: (blk0 + b, 0)),
            pl.BlockSpec((None, LEFT_FRAMES, D_ATTN), lambda b: (b, 0, 0)),
            pl.BlockSpec((CHUNK, D_ATTN), lambda b: (blk0 + b, 1)),
            pl.BlockSpec((None, LEFT_FRAMES, D_ATTN), lambda b: (b, 0, 0)),
            pl.BlockSpec((CHUNK, D_ATTN), lambda b: (blk0 + b, 2)),
            pl.BlockSpec((HEAD_PAIRS, 2 * CHUNK, BAND), lambda b: (0, 0, 0)),
        ],
        out_specs=[
            pl.BlockSpec((CHUNK, D_ATTN), lambda b: (b, 0)),
            pl.BlockSpec((None, LEFT_FRAMES, D_ATTN), lambda b: (b, 0, 0)),
            pl.BlockSpec((None, LEFT_FRAMES, D_ATTN), lambda b: (b, 0, 0)),
        ],
        out_shape=[jax.ShapeDtypeStruct((dec_batch * CHUNK, D_ATTN), BF16), state, state],
        scratch_shapes=[pltpu.VMEM((BAND, D_ATTN), BF16), pltpu.VMEM((BAND, D_ATTN), BF16)],
        compiler_params=_params(("arbitrary",)),
        name="attention_sample",
    )(qkv, cache_k, qkv, cache_v, qkv, bias2)


CONV_ROWS = 32


def _conv_kernel(prev_ref, u_ref, w_ref, b_ref, g_ref, be_ref, o_ref, up_ref, *, rows, zero_first):
    prev = prev_ref[...]
    if zero_first:
        prev = jnp.where(pl.program_id(1) == 0, jnp.zeros_like(prev), prev)
    up_ref[0:CONV_HALO, :] = prev
    up_ref[CONV_HALO:CONV_HALO + rows, :] = u_ref[...]
    lead = CONV_HALO - (CONV_WIDTH - 1)
    for r in range(rows // CONV_ROWS):
        base = r * CONV_ROWS + lead
        acc = jnp.broadcast_to(b_ref[...], (CONV_ROWS, D_CONV))
        for j in range(CONV_WIDTH):
            acc = acc + w_ref[j:j + 1, :] * up_ref[base + j:base + j + CONV_ROWS, :]
        y = _layer_norm(acc, g_ref[...], be_ref[...])
        o_ref[r * CONV_ROWS:(r + 1) * CONV_ROWS, :] = (y * jax.nn.sigmoid(y)).astype(o_ref.dtype)


def _conv_specs_small():
    full = lambda *_: (0, 0)
    return [
        pl.BlockSpec((CONV_WIDTH, D_CONV), full),
        pl.BlockSpec((1, D_CONV), full),
        pl.BlockSpec((1, D_CONV), full),
        pl.BlockSpec((1, D_CONV), full),
    ]


def _conv_prompt(u, dw_w, dw_b, ln_g, ln_b, batch, seq):
    rows = _pick_tile(seq, (256, 128, 64))
    nsb = seq // rows
    per = rows // CONV_HALO
    halo = lambda b, i: (jnp.maximum((b * nsb + i) * per - 1, 0), 0)
    return pl.pallas_call(
        functools.partial(_conv_kernel, rows=rows, zero_first=True),
        grid=(batch, nsb),
        in_specs=[
            pl.BlockSpec((CONV_HALO, D_CONV), halo),
            pl.BlockSpec((rows, D_CONV), lambda b, i: (b * nsb + i, 0)),
        ] + _conv_specs_small(),
        out_specs=pl.BlockSpec((rows, D_CONV), lambda b, i: (b * nsb + i, 0)),
        out_shape=jax.ShapeDtypeStruct((batch * seq, D_CONV), BF16),
        scratch_shapes=[pltpu.VMEM((CONV_HALO + rows, D_CONV), F32)],
        compiler_params=_params(("arbitrary", "arbitrary")),
        name="conv_prompt",
    )(u, u, dw_w, dw_b, ln_g, ln_b)


def _conv_sample(u, halo, dw_w, dw_b, ln_g, ln_b, row0, dec_batch):
    blk0 = row0 // CHUNK
    return pl.pallas_call(
        functools.partial(_conv_kernel, rows=CHUNK, zero_first=False),
        grid=(dec_batch,),
        in_specs=[
            pl.BlockSpec((None, CONV_HALO, D_CONV), lambda b: (b, 0, 0)),
            pl.BlockSpec((CHUNK, D_CONV), lambda b: (blk0 + b, 0)),
        ] + _conv_specs_small(),
        out_specs=pl.BlockSpec((CHUNK, D_CONV), lambda b: (b, 0)),
        out_shape=jax.ShapeDtypeStruct((dec_batch * CHUNK, D_CONV), BF16),
        scratch_shapes=[pltpu.VMEM((CONV_HALO + CHUNK, D_CONV), F32)],
        compiler_params=_params(("arbitrary",)),
        name="conv_sample",
    )(halo, u, dw_w, dw_b, ln_g, ln_b)


ROUTER_ROWS = 32


def _first_max(vals):
    m = vals[0]
    for v in vals[1:]:
        m = jnp.maximum(m, v)
    hots, taken = [], None
    for v in vals:
        hit = v == m
        if taken is None:
            hots.append(hit)
            taken = hit
        else:
            hots.append(jnp.logical_and(hit, jnp.logical_not(taken)))
            taken = jnp.logical_or(taken, hit)
    return m, hots


def _merge_kernel(x_ref, a_ref, c_ref, ga_ref, gc_ref, wao_ref, wco_ref, wo_ref, g1_ref, b1_ref, wr_ref, br_ref,
                  h_ref, ri_ref, rw_ref, *, alpha):
    a_br = jnp.dot(a_ref[...], wao_ref[...], preferred_element_type=F32)
    c_br = jnp.dot(c_ref[...], wco_ref[...], preferred_element_type=F32)
    mix = ga_ref[...].astype(F32) * a_br + gc_ref[...].astype(F32) * c_br
    m = jnp.dot(mix.astype(BF16), wo_ref[...], preferred_element_type=F32)
    h = _layer_norm(alpha * x_ref[...] + m, g1_ref[...], b1_ref[...])
    h_ref[...] = h

    lt = lax.dot_general(wr_ref[...], h, (((1,), (1,)), ((), ())), precision=lax.Precision.HIGHEST,
                         preferred_element_type=F32) + br_ref[...]
    gl = [lt[g:g + 1, :] for g in range(N_GROUPS)]
    gmax, gsel = _first_max(gl)
    denom = jnp.exp(gl[0] - gmax)
    for g in range(1, N_GROUPS):
        denom = denom + jnp.exp(gl[g] - gmax)
    p_group = 1.0 / denom
    zero = jnp.zeros_like(gmax)
    el = []
    for e in range(EXPERTS_PER_GROUP):
        v = zero
        for g in range(N_GROUPS):
            row = N_GROUPS + g * EXPERTS_PER_GROUP + e
            v = jnp.where(gsel[g], lt[row:row + 1, :], v)
        el.append(v)
    m1, hot1 = _first_max(el)
    ninf = jnp.full_like(m1, -jnp.inf)
    m2, hot2 = _first_max([jnp.where(hot1[e], ninf, el[e]) for e in range(EXPERTS_PER_GROUP)])
    hot2 = [jnp.logical_and(hot2[e], jnp.logical_not(hot1[e])) for e in range(EXPERTS_PER_GROUP)]
    t = jnp.exp(m2 - m1)
    w1 = p_group / (1.0 + t)
    w2 = p_group * t / (1.0 + t)
    izero = jnp.zeros(gmax.shape, jnp.int32)
    gid = izero
    for g in range(1, N_GROUPS):
        gid = jnp.where(gsel[g], g, gid)
    i1, i2 = izero, izero
    for e in range(1, EXPERTS_PER_GROUP):
        i1 = jnp.where(hot1[e], e, i1)
        i2 = jnp.where(hot2[e], e, i2)
    ri_ref[...] = jnp.zeros_like(ri_ref)
    rw_ref[...] = jnp.zeros_like(rw_ref)
    ri_ref[0:1, :] = gid * EXPERTS_PER_GROUP + i1
    ri_ref[1:2, :] = gid * EXPERTS_PER_GROUP + i2
    rw_ref[0:1, :] = w1
    rw_ref[1:2, :] = w2


def _merge(x, attn, conv, gates, w_ao, w_co, w_o, ln_g, ln_b, w_router_t, b_router, alpha):
    n = x.shape[0]
    tm = _pick_tile(n, (512, 256, 128))
    full = lambda i: (0, 0)
    once = pl.Buffered(1)
    return pl.pallas_call(
        functools.partial(_merge_kernel, alpha=alpha),
        grid=(n // tm,),
        in_specs=[
            pl.BlockSpec((tm, D_MODEL), lambda i: (i, 0)),
            pl.BlockSpec((tm, D_ATTN), lambda i: (i, 0)),
            pl.BlockSpec((tm, D_CONV), lambda i: (i, 0)),
            pl.BlockSpec((tm, D_MODEL), lambda i: (i, 0)),
            pl.BlockSpec((tm, D_MODEL), lambda i: (i, 1)),
            pl.BlockSpec((D_ATTN, D_MODEL), full, pipeline_mode=once),
            pl.BlockSpec((D_CONV, D_MODEL), full, pipeline_mode=once),
            pl.BlockSpec((D_MODEL, D_MODEL), full, pipeline_mode=once),
            pl.BlockSpec((1, D_MODEL), full),
            pl.BlockSpec((1, D_MODEL), full),
            pl.BlockSpec((ROUTER_ROWS, D_MODEL), full),
            pl.BlockSpec((ROUTER_ROWS, 1), full),
        ],
        out_specs=[
            pl.BlockSpec((tm, D_MODEL), lambda i: (i, 0)),
            pl.BlockSpec((8, tm), lambda i: (0, i)),
            pl.BlockSpec((8, tm), lambda i: (0, i)),
        ],
        out_shape=[
            jax.ShapeDtypeStruct((n, D_MODEL), F32),
            jax.ShapeDtypeStruct((8, n), jnp.int32),
            jax.ShapeDtypeStruct((8, n), F32),
        ],
        compiler_params=_params(("arbitrary",)),
        name="merge_ln1_router",
    )(x, attn, conv, gates, gates, w_ao, w_co, w_o, ln_g, ln_b, w_router_t, b_router)


def _row_copy(src_hbm, row, dst, sem):
    return pltpu.make_async_copy(src_hbm.at[pl.ds(row, 1)], dst, sem)


def _moe_kernel(te_ref, tv_ref, src_ref, nxt_ref, h_hbm, wg_ref, wu_ref, wd_ref, y_ref, xbuf, sem):
    t = pl.program_id(0)
    nt = pl.num_programs(0)
    slot = lax.rem(t, 2)

    def gather(idx_ref, sl):
        def body(r, carry):
            _row_copy(h_hbm, idx_ref[0, 0, r], xbuf.at[sl, pl.ds(r, 1)], sem.at[sl]).start()
            return carry
        lax.fori_loop(0, MOE_TILE, body, 0, unroll=8)

    def drain(sl):
        def body(r, carry):
            _row_copy(h_hbm, 0, xbuf.at[sl, pl.ds(r, 1)], sem.at[sl]).wait()
            return carry
        lax.fori_loop(0, MOE_TILE, body, 0, unroll=8)

    @pl.when(t == 0)
    def _():
        gather(src_ref, 0)

    nxt = jnp.minimum(t + 1, nt - 1)

    @pl.when(jnp.logical_and(t + 1 < nt, tv_ref[nxt] == 1))
    def _():
        gather(nxt_ref, 1 - slot)

    @pl.when(tv_ref[t] == 1)
    def _():
        drain(slot)
        x = xbuf[slot].astype(BF16)
        gate = jnp.dot(x, wg_ref[...], preferred_element_type=F32)
        up = jnp.dot(x, wu_ref[...], preferred_element_type=F32)
        hid = (gate * jax.nn.sigmoid(gate) * up).astype(BF16)
        y_ref[...] = jnp.dot(hid, wd_ref[...], preferred_element_type=F32)

    @pl.when(tv_ref[t] == 0)
    def _():
        y_ref[...] = jnp.zeros_like(y_ref)


def _moe(h, tile_expert, tile_valid, src_rows, w_gate, w_up, w_down):
    n_tiles = src_rows.shape[0]
    grid_spec = pltpu.PrefetchScalarGridSpec(
        num_scalar_prefetch=2,
        grid=(n_tiles,),
        in_specs=[
            pl.BlockSpec((1, 1, MOE_TILE), lambda t, te, tv: (t, 0, 0), memory_space=pltpu.SMEM),
            pl.BlockSpec((1, 1, MOE_TILE), lambda t, te, tv: (jnp.minimum(t + 1, n_tiles - 1), 0, 0),
                         memory_space=pltpu.SMEM),
            pl.BlockSpec(memory_space=pl.ANY),
            pl.BlockSpec((None, D_MODEL, D_EXPERT), lambda t, te, tv: (te[t], 0, 0)),
            pl.BlockSpec((None, D_MODEL, D_EXPERT), lambda t, te, tv: (te[t], 0, 0)),
            pl.BlockSpec((None, D_EXPERT, D_MODEL), lambda t, te, tv: (te[t], 0, 0)),
        ],
        out_specs=pl.BlockSpec((MOE_TILE, D_MODEL), lambda t, te, tv: (t, 0)),
        scratch_shapes=[pltpu.VMEM((2, MOE_TILE, D_MODEL), F32), pltpu.SemaphoreType.DMA((2,))],
    )
    return pl.pallas_call(
        _moe_kernel,
        grid_spec=grid_spec,
        out_shape=jax.ShapeDtypeStruct((n_tiles * MOE_TILE, D_MODEL), F32),
        compiler_params=_params(("arbitrary",)),
        name="routed_experts",
    )(tile_expert, tile_valid, src_rows, src_rows, h, w_gate, w_up, w_down)


def _combine_kernel(pos_ref, nxt_ref, h_ref, w_ref, g2_ref, b2_ref, y_hbm, o_ref, ybuf, sem, *, alpha, rows):
    i = pl.program_id(0)
    n = pl.num_programs(0)
    slot = lax.rem(i, 2)

    def gather(idx_ref, sl):
        def body(r, carry):
            for k in range(2):
                _row_copy(y_hbm, idx_ref[0, k, r], ybuf.at[sl, k, pl.ds(r, 1)], sem.at[sl]).start()
            return carry
        lax.fori_loop(0, rows, body, 0, unroll=4)

    def drain(sl):
        def body(r, carry):
            for k in range(2):
                _row_copy(y_hbm, 0, ybuf.at[sl, k, pl.ds(r, 1)], sem.at[sl]).wait()
            return carry
        lax.fori_loop(0, rows, body, 0, unroll=4)

    @pl.when(i == 0)
    def _():
        gather(pos_ref, 0)

    @pl.when(i + 1 < n)
    def _():
        gather(nxt_ref, 1 - slot)

    drain(slot)
    w = w_ref[...]
    moe = w[:, 0:1] * ybuf[slot, 0] + w[:, 1:2] * ybuf[slot, 1]
    o_ref[...] = _layer_norm(alpha * h_ref[...] + moe, g2_ref[...], b2_ref[...])


def _combine(h, y_grouped, pos_blocks, w_cols, ln_g, ln_b, alpha):
    n = h.shape[0]
    n_blk, _, rows = pos_blocks.shape
    full = lambda i: (0, 0)
    return pl.pallas_call(
        functools.partial(_combine_kernel, alpha=alpha, rows=rows),
        grid=(n_blk,),
        in_specs=[
            pl.BlockSpec((1, 2, rows), lambda i: (i, 0, 0), memory_space=pltpu.SMEM),
            pl.BlockSpec((1, 2, rows), lambda i: (jnp.minimum(i + 1, n_blk - 1), 0, 0), memory_space=pltpu.SMEM),
            pl.BlockSpec((rows, D_MODEL), lambda i: (i, 0)),
            pl.BlockSpec((rows, 2), lambda i: (i, 0)),
            pl.BlockSpec((1, D_MODEL), full),
            pl.BlockSpec((1, D_MODEL), full),
            pl.BlockSpec(memory_space=pl.ANY),
        ],
        out_specs=pl.BlockSpec((rows, D_MODEL), lambda i: (i, 0)),
        out_shape=jax.ShapeDtypeStruct((n, D_MODEL), F32),
        scratch_shapes=[pltpu.VMEM((2, 2, rows, D_MODEL), F32), pltpu.SemaphoreType.DMA((2,))],
        compiler_params=_params(("arbitrary",)),
        name="combine_ln2",
    )(pos_blocks, pos_blocks, h, w_cols, ln_g, ln_b, y_grouped)


def _routing_tables(expert_ids, n_tok, rows):
    flat = expert_ids.reshape(-1)
    onehot = (flat[:, None] == jnp.arange(N_EXPERTS, dtype=jnp.int32)[None, :]).astype(jnp.int32)
    csum = jnp.cumsum(onehot, axis=0)
    rank = jnp.sum(csum * onehot, axis=1) - 1
    count = csum[-1]
    size = (count + MOE_TILE - 1) // MOE_TILE * MOE_TILE
    end = jnp.cumsum(size)
    pos = (end - size)[flat] + rank
    n_tiles = (2 * n_tok) // MOE_TILE + N_EXPERTS
    tile_start = jnp.arange(n_tiles, dtype=jnp.int32) * MOE_TILE
    tile_expert = jnp.minimum(jnp.sum((tile_start[:, None] >= end[None, :]).astype(jnp.int32), axis=1), N_EXPERTS - 1)
    tile_valid = (tile_start < end[-1]).astype(jnp.int32)
    token = jnp.tile(jnp.arange(n_tok, dtype=jnp.int32), 2)
    src = jnp.zeros((n_tiles * MOE_TILE,), jnp.int32).at[pos].set(token, unique_indices=True)
    pos_blocks = pos.reshape(2, n_tok // rows, rows).transpose(1, 0, 2)
    return tile_expert.astype(jnp.int32), tile_valid, src.reshape(n_tiles, 1, MOE_TILE), pos_blocks


def _rel_bias_pairs(rel_table):
    qi = jnp.arange(CHUNK)[:, None]
    kj = jnp.arange(BAND)[None, :]
    idx = jnp.clip(qi + LEFT_FRAMES - kj, -MAX_REL, MAX_REL) + MAX_REL
    return rel_table[:, idx].reshape(HEAD_PAIRS, 2 * CHUNK, BAND)


def _permute_in_columns(a):
    q = 3 * D_ATTN
    half = D_CONV // 2
    ua, ub = a[..., q:q + D_CONV], a[..., q + D_CONV:q + 2 * D_CONV]
    return jnp.concatenate([a[..., :q], ua[..., :half], ub[..., :half], ua[..., half:], ub[..., half:],
                            a[..., q + 2 * D_CONV:]], axis=-1)


def _encoder_layer(x_all, n_prompt, batch, seq, dec_batch, cache_k, cache_v, cache_conv, alpha,
                   w_in, b_in, rel_table, dw_w, dw_b, conv_ln_g, conv_ln_b, w_attn_out, w_conv_out, w_o,
                   ln1_g, ln1_b, w_rg, b_rg, w_re, b_re, w_gate, w_up, w_down, ln2_g, ln2_b):
    n_tok = x_all.shape[0]
    row = lambda v: v.reshape(1, -1).astype(F32)

    qkv, u, gates = _in_projection(x_all, _permute_in_columns(w_in).astype(BF16), row(_permute_in_columns(b_in)))

    bias2 = _rel_bias_pairs(rel_table.astype(F32))
    attn_p = _attention_prompt(qkv, bias2, batch, seq)
    attn_s, k_state_s, v_state_s = _attention_sample(
        qkv, cache_k.reshape(dec_batch, LEFT_FRAMES, D_ATTN), cache_v.reshape(dec_batch, LEFT_FRAMES, D_ATTN),
        bias2, n_prompt, dec_batch)

    conv_p = _conv_prompt(u, dw_w, row(dw_b), row(conv_ln_g), row(conv_ln_b), batch, seq)
    halo = jnp.pad(cache_conv, ((0, 0), (CONV_HALO - (CONV_WIDTH - 1), 0), (0, 0)))
    conv_s = _conv_sample(u, halo, dw_w, row(dw_b), row(conv_ln_g), row(conv_ln_b), n_prompt, dec_batch)

    w_router_t = jnp.concatenate([w_rg, w_re.reshape(D_MODEL, N_EXPERTS)], axis=1).T
    w_router_t = jnp.pad(w_router_t, ((0, ROUTER_ROWS - N_GROUPS - N_EXPERTS), (0, 0)))
    b_router = jnp.pad(jnp.concatenate([b_rg, b_re.reshape(-1)]), (0, ROUTER_ROWS - N_GROUPS - N_EXPERTS))
    h, route_idx, route_w = _merge(
        x_all, jnp.concatenate([attn_p, attn_s]), jnp.concatenate([conv_p, conv_s]), gates,
        w_attn_out.astype(BF16), w_conv_out.astype(BF16), w_o.astype(BF16), row(ln1_g), row(ln1_b),
        w_router_t.astype(F32), b_router.reshape(-1, 1).astype(F32), alpha)

    rows = _pick_tile(n_tok, (256, 128, 64))
    tile_expert, tile_valid, src_rows, pos_blocks = _routing_tables(route_idx[0:2], n_tok, rows)
    y_grouped = _moe(h, tile_expert, tile_valid, src_rows, w_gate.astype(BF16), w_up.astype(BF16),
                     w_down.astype(BF16))
    y_all = _combine(h, y_grouped, pos_blocks, route_w[0:2].T, row(ln2_g), row(ln2_b), alpha)

    k3 = qkv.reshape(-1, 3, N_HEADS, HEAD_DIM)
    keep = min(LEFT_FRAMES, seq)
    kv_p = k3[:n_prompt].reshape(batch, seq, 3, N_HEADS, HEAD_DIM)[:, seq - keep:]
    conv_state_p = u[:n_prompt].reshape(batch, seq, D_CONV)[:, seq - (CONV_WIDTH - 1):]
    conv_state_s = u[n_prompt:].reshape(dec_batch, CHUNK, D_CONV)[:, CHUNK - (CONV_WIDTH - 1):]
    states = (kv_p[:, :, 1], kv_p[:, :, 2], conv_state_p,
              k_state_s.reshape(dec_batch, LEFT_FRAMES, N_HEADS, HEAD_DIM),
              v_state_s.reshape(dec_batch, LEFT_FRAMES, N_HEADS, HEAD_DIM), conv_state_s)
    return y_all, states


def kernel(x_prompt, x_sample, cache_attn_k, cache_attn_v, cache_conv, w_in, b_in, rel_table, dw_w, dw_b,
           conv_ln_g, conv_ln_b, w_attn_out, w_conv_out, w_o, ln1_g, ln1_b, w_router_group, b_router_group,
           w_router_expert, b_router_expert, w_gate, w_up, w_down, ln2_g, ln2_b):
    depth = w_in.shape[0]
    batch, seq, _ = x_prompt.shape
    dec_batch, dec_seq, _ = x_sample.shape
    assert dec_seq == CHUNK and cache_attn_k.shape[2] == LEFT_FRAMES and seq % LEFT_FRAMES == 0
    assert cache_conv.shape[2] == CONV_WIDTH - 1
    alpha = (2.0 * depth) ** 0.25
    n_prompt = batch * seq
    x_all = jnp.concatenate([x_prompt.reshape(n_prompt, D_MODEL), x_sample.reshape(dec_batch * dec_seq, D_MODEL)])
    collected = [[] for _ in range(6)]
    for l in range(depth):
        x_all, states = _encoder_layer(
            x_all, n_prompt, batch, seq, dec_batch, cache_attn_k[l], cache_attn_v[l], cache_conv[l], alpha,
            w_in[l], b_in[l], rel_table[l], dw_w[l], dw_b[l], conv_ln_g[l], conv_ln_b[l], w_attn_out[l],
            w_conv_out[l], w_o[l], ln1_g[l], ln1_b[l], w_router_group[l], b_router_group[l], w_router_expert[l],
            b_router_expert[l], w_gate[l], w_up[l], w_down[l], ln2_g[l], ln2_b[l])
        for acc, s in zip(collected, states):
            acc.append(s)
    y_prompt = x_all[:n_prompt].reshape(batch, seq, D_MODEL)
    y_sample = x_all[n_prompt:].reshape(dec_batch, dec_seq, D_MODEL)
    return (y_prompt, y_sample) + tuple(jnp.stack(c) for c in collected)
```

```python
import functools

import jax
import jax.numpy as jnp
from jax import lax
from jax.experimental import pallas as pl
from jax.experimental.pallas import tpu as pltpu

F32 = jnp.float32
BF16 = jnp.bfloat16
I32 = jnp.int32

D_MODEL = 2048
N_HEADS = 16
HEAD_DIM = 64
D_ATTN = N_HEADS * HEAD_DIM
CHUNK = 64
LEFT_FRAMES = 512
BAND = LEFT_FRAMES + CHUNK
MAX_REL = 128
D_CONV = 1024
CONV_WIDTH = 31
N_GROUPS = 4
EXPERTS_PER_GROUP = 4
N_EXPERTS = N_GROUPS * EXPERTS_PER_GROUP
D_EXPERT = 512
LN_EPS = 1e-5

LANES = 128
SUBLANES = 8
HEAD_PAIRS = D_ATTN // LANES
CONV_HALO = 32
MOE_TILE = 256
VMEM_LIMIT = 56 * 1024 * 1024


def _params(sem, vmem=VMEM_LIMIT):
    return pltpu.CompilerParams(dimension_semantics=sem, vmem_limit_bytes=vmem)


def _pick_tile(sizes, prefs):
    for t in prefs:
        if all(n % t == 0 for n in sizes):
            return t
    raise ValueError(f"no tile in {prefs} divides {sizes}")


def _layer_norm(x, g, b):
    mu = jnp.mean(x, axis=-1, keepdims=True)
    xc = x - mu
    var = jnp.mean(xc * xc, axis=-1, keepdims=True)
    return xc * lax.rsqrt(var + LN_EPS) * g + b


def _two_part(n_first_blocks):
    first = lambda i: jnp.minimum(i, n_first_blocks - 1)
    second = lambda i: jnp.maximum(i - n_first_blocks, 0)
    return first, second


IN_STEPS = 8
IN_QKV = 3 * D_ATTN // IN_STEPS
IN_GLU = D_CONV // IN_STEPS
IN_GATE = 2 * D_MODEL // IN_STEPS
IN_COLS = IN_QKV + 2 * IN_GLU + IN_GATE


def _inproj_kernel(xp_ref, xs_ref, w_ref, b_ref, qkv_ref, u_ref, g_ref, xb_ref, *, prompt_blocks):
    i = pl.program_id(0)
    j = pl.program_id(1)

    @pl.when(jnp.logical_and(j == 0, i < prompt_blocks))
    def _():
        xb_ref[...] = xp_ref[...].astype(BF16)

    @pl.when(jnp.logical_and(j == 0, i >= prompt_blocks))
    def _():
        xb_ref[...] = xs_ref[...].astype(BF16)

    z = jnp.dot(xb_ref[...], w_ref[...], preferred_element_type=F32) + b_ref[...]
    qkv_ref[...] = z[:, :IN_QKV]
    u_ref[...] = z[:, IN_QKV:IN_QKV + IN_GLU] * jax.nn.sigmoid(z[:, IN_QKV + IN_GLU:IN_QKV + 2 * IN_GLU])
    g_ref[...] = jax.nn.sigmoid(z[:, IN_QKV + 2 * IN_GLU:]).astype(BF16)


def _permute_in_columns(a):
    q = 3 * D_ATTN
    lead = a.shape[:-1]
    parts = [a[..., :q].reshape(*lead, IN_STEPS, IN_QKV),
             a[..., q:q + D_CONV].reshape(*lead, IN_STEPS, IN_GLU),
             a[..., q + D_CONV:q + 2 * D_CONV].reshape(*lead, IN_STEPS, IN_GLU),
             a[..., q + 2 * D_CONV:].reshape(*lead, IN_STEPS, IN_GATE)]
    return jnp.concatenate(parts, axis=-1).reshape(*lead, IN_STEPS * IN_COLS)


def _in_projection(x_p, x_s, w_perm, b_perm):
    n_p, n_s = x_p.shape[0], x_s.shape[0]
    n = n_p + n_s
    tm = _pick_tile((n_p, n_s), (1024, 512, 256, 128, 64))
    pb = n_p // tm
    first, second = _two_part(pb)
    once = pl.Buffered(1)
    return pl.pallas_call(
        functools.partial(_inproj_kernel, prompt_blocks=pb),
        grid=(n // tm, IN_STEPS),
        in_specs=[
            pl.BlockSpec((tm, D_MODEL), lambda i, j: (first(i), 0), pipeline_mode=once),
            pl.BlockSpec((tm, D_MODEL), lambda i, j: (second(i), 0), pipeline_mode=once),
            pl.BlockSpec((D_MODEL, IN_COLS), lambda i, j: (0, j)),
            pl.BlockSpec((1, IN_COLS), lambda i, j: (0, j)),
        ],
        out_specs=[
            pl.BlockSpec((tm, IN_QKV), lambda i, j: (i, j)),
            pl.BlockSpec((tm, IN_GLU), lambda i, j: (i, j)),
            pl.BlockSpec((tm, IN_GATE), lambda i, j: (i, j)),
        ],
        out_shape=[
            jax.ShapeDtypeStruct((n, 3 * D_ATTN), F32),
            jax.ShapeDtypeStruct((n, D_CONV), F32),
            jax.ShapeDtypeStruct((n, 2 * D_MODEL), BF16),
        ],
        scratch_shapes=[pltpu.VMEM((tm, D_MODEL), BF16)],
        compiler_params=_params(("arbitrary", "arbitrary")),
        name="in_projection",
    )(x_p, x_s, w_perm, b_perm)


def _attn_body(q_ref, kp_ref, kc_ref, vp_ref, vc_ref, bias_ref, o_ref, kk_ref, vv_ref, *, n_chunks, first_pos):
    cur = n_chunks * CHUNK
    kk_ref[0:LEFT_FRAMES, :] = kp_ref[...].astype(BF16)
    kk_ref[LEFT_FRAMES:LEFT_FRAMES + cur, :] = kc_ref[...].astype(BF16)
    vv_ref[0:LEFT_FRAMES, :] = vp_ref[...].astype(BF16)
    vv_ref[LEFT_FRAMES:LEFT_FRAMES + cur, :] = vc_ref[...].astype(BF16)

    lane = lax.broadcasted_iota(I32, (CHUNK, LANES), 1)
    low = lane < HEAD_DIM
    key_idx = lax.broadcasted_iota(I32, (1, BAND), 1)
    neg = jnp.finfo(F32).min

    def chunk(c, carry):
        r0 = pl.multiple_of(c * CHUNK, CHUNK)
        for p in range(HEAD_PAIRS):
            cols = slice(p * LANES, (p + 1) * LANES)
            q2 = (q_ref[pl.ds(r0, CHUNK), cols] * (HEAD_DIM ** -0.5)).astype(BF16)
            zero = jnp.zeros_like(q2)
            qq = jnp.concatenate([jnp.where(low, q2, zero), jnp.where(low, zero, q2)], axis=0)
            kb = kk_ref[pl.ds(r0, BAND), cols]
            s = lax.dot_general(qq, kb, (((1,), (1,)), ((), ())), preferred_element_type=F32)
            s = s + bias_ref[p]
            if first_pos is not None:
                s = jnp.where(key_idx + (first_pos + r0) >= 0, s, neg)
            m = jnp.max(s, axis=-1, keepdims=True)
            e = jnp.exp(s - m)
            l = jnp.sum(e, axis=-1, keepdims=True)
            vb = vv_ref[pl.ds(r0, BAND), cols]
            o2 = jnp.dot(e.astype(BF16), vb, preferred_element_type=F32) / l
            o = jnp.where(low, o2[:CHUNK], o2[CHUNK:])
            o_ref[pl.ds(r0, CHUNK), cols] = o.astype(o_ref.dtype)
        return carry

    lax.fori_loop(0, n_chunks, chunk, 0)


def _attn_prompt_kernel(q_ref, kp_ref, kc_ref, vp_ref, vc_ref, bias_ref, o_ref, kk_ref, vv_ref, *, n_chunks):
    first_pos = (pl.program_id(1) - 1) * LEFT_FRAMES
    _attn_body(q_ref, kp_ref, kc_ref, vp_ref, vc_ref, bias_ref, o_ref, kk_ref, vv_ref,
               n_chunks=n_chunks, first_pos=first_pos)


def _attn_sample_kernel(q_ref, kp_ref, kc_ref, vp_ref, vc_ref, bias_ref, o_ref, ks_ref, vs_ref, kk_ref, vv_ref):
    _attn_body(q_ref, kp_ref, kc_ref, vp_ref, vc_ref, bias_ref, o_ref, kk_ref, vv_ref,
               n_chunks=1, first_pos=None)
    keep = LEFT_FRAMES - CHUNK
    ks_ref[0:keep, :] = kp_ref[CHUNK:, :]
    ks_ref[keep:, :] = kc_ref[...]
    vs_ref[0:keep, :] = vp_ref[CHUNK:, :]
    vs_ref[keep:, :] = vc_ref[...]


def _attention_prompt(qkv, bias2, batch, seq):
    blk = LEFT_FRAMES
    nsb = seq // blk
    row = lambda b, i: b * nsb + i
    prev = lambda b, i: b * nsb + jnp.maximum(i - 1, 0)
    return pl.pallas_call(
        functools.partial(_attn_prompt_kernel, n_chunks=blk // CHUNK),
        grid=(batch, nsb),
        in_specs=[
            pl.BlockSpec((blk, D_ATTN), lambda b, i: (row(b, i), 0)),
            pl.BlockSpec((blk, D_ATTN), lambda b, i: (prev(b, i), 1)),
            pl.BlockSpec((blk, D_ATTN), lambda b, i: (row(b, i), 1)),
            pl.BlockSpec((blk, D_ATTN), lambda b, i: (prev(b, i), 2)),
            pl.BlockSpec((blk, D_ATTN), lambda b, i: (row(b, i), 2)),
            pl.BlockSpec((HEAD_PAIRS, 2 * CHUNK, BAND), lambda b, i: (0, 0, 0)),
        ],
        out_specs=pl.BlockSpec((blk, D_ATTN), lambda b, i: (row(b, i), 0)),
        out_shape=jax.ShapeDtypeStruct((batch * seq, D_ATTN), BF16),
        scratch_shapes=[pltpu.VMEM((2 * blk, D_ATTN), BF16), pltpu.VMEM((2 * blk, D_ATTN), BF16)],
        compiler_params=_params(("arbitrary", "arbitrary")),
        name="attention_prompt",
    )(qkv, qkv, qkv, qkv, qkv, bias2)


def _attention_sample(qkv, cache_k, cache_v, bias2, row0, dec_batch):
    blk0 = row0 // CHUNK
    state = jax.ShapeDtypeStruct((dec_batch, LEFT_FRAMES, D_ATTN), F32)
    return pl.pallas_call(
        _attn_sample_kernel,
        grid=(dec_batch,),
        in_specs=[
            pl.BlockSpec((CHUNK, D_ATTN), lambda b: (blk0 + b, 0)),
            pl.BlockSpec((None, LEFT_FRAMES, D_ATTN), lambda b: (b, 0, 0)),
            pl.BlockSpec((CHUNK, D_ATTN), lambda b: (blk0 + b, 1)),
            pl.BlockSpec((None, LEFT_FRAMES, D_ATTN), lambda b: (b, 0, 0)),
            pl.BlockSpec((CHUNK, D_ATTN), lambda b: (blk0 + b, 2)),
            pl.BlockSpec((HEAD_PAIRS, 2 * CHUNK, BAND), lambda b: (0, 0, 0)),
        ],
        out_specs=[
            pl.BlockSpec((CHUNK, D_ATTN), lambda b: (b, 0)),
            pl.BlockSpec((None, LEFT_FRAMES, D_ATTN), lambda b: (b, 0, 0)),
            pl.BlockSpec((None, LEFT_FRAMES, D_ATTN), lambda b: (b, 0, 0)),
        ],
        out_shape=[jax.ShapeDtypeStruct((dec_batch * CHUNK, D_ATTN), BF16), state, state],
        scratch_shapes=[pltpu.VMEM((BAND, D_ATTN), BF16), pltpu.VMEM((BAND, D_ATTN), BF16)],
        compiler_params=_params(("arbitrary",)),
        name="attention_sample",
    )(qkv, cache_k, qkv, cache_v, qkv, bias2)


CONV_ROWS = 32


def _conv_kernel(prev_ref, u_ref, w_ref, b_ref, g_ref, be_ref, o_ref, up_ref, *, rows, zero_first):
    prev = prev_ref[...]
    if zero_first:
        prev = jnp.where(pl.program_id(1) == 0, jnp.zeros_like(prev), prev)
    up_ref[0:CONV_HALO, :] = prev
    up_ref[CONV_HALO:CONV_HALO + rows, :] = u_ref[...]
    lead = CONV_HALO - (CONV_WIDTH - 1)
    for r in range(rows // CONV_ROWS):
        base = r * CONV_ROWS + lead
        acc = jnp.broadcast_to(b_ref[...], (CONV_ROWS, D_CONV))
        for j in range(CONV_WIDTH):
            acc = acc + w_ref[j:j + 1, :] * up_ref[base + j:base + j + CONV_ROWS, :]
        y = _layer_norm(acc, g_ref[...], be_ref[...])
        o_ref[r * CONV_ROWS:(r + 1) * CONV_ROWS, :] = (y * jax.nn.sigmoid(y)).astype(o_ref.dtype)


def _conv_specs_small():
    full = lambda *_: (0, 0)
    return [
        pl.BlockSpec((CONV_WIDTH, D_CONV), full),
        pl.BlockSpec((1, D_CONV), full),
        pl.BlockSpec((1, D_CONV), full),
        pl.BlockSpec((1, D_CONV), full),
    ]


def _conv_prompt(u, dw_w, dw_b, ln_g, ln_b, batch, seq):
    rows = _pick_tile((seq,), (256, 128, 64))
    nsb = seq // rows
    per = rows // CONV_HALO
    halo = lambda b, i: (jnp.maximum((b * nsb + i) * per - 1, 0), 0)
    return pl.pallas_call(
        functools.partial(_conv_kernel, rows=rows, zero_first=True),
        grid=(batch, nsb),
        in_specs=[
            pl.BlockSpec((CONV_HALO, D_CONV), halo),
            pl.BlockSpec((rows, D_CONV), lambda b, i: (b * nsb + i, 0)),
        ] + _conv_specs_small(),
        out_specs=pl.BlockSpec((rows, D_CONV), lambda b, i: (b * nsb + i, 0)),
        out_shape=jax.ShapeDtypeStruct((batch * seq, D_CONV), BF16),
        scratch_shapes=[pltpu.VMEM((CONV_HALO + rows, D_CONV), F32)],
        compiler_params=_params(("arbitrary", "arbitrary")),
        name="conv_prompt",
    )(u, u, dw_w, dw_b, ln_g, ln_b)


def _conv_sample(u, halo, dw_w, dw_b, ln_g, ln_b, row0, dec_batch):
    blk0 = row0 // CHUNK
    return pl.pallas_call(
        functools.partial(_conv_kernel, rows=CHUNK, zero_first=False),
        grid=(dec_batch,),
        in_specs=[
            pl.BlockSpec((None, CONV_HALO, D_CONV), lambda b: (b, 0, 0)),
            pl.BlockSpec((CHUNK, D_CONV), lambda b: (blk0 + b, 0)),
        ] + _conv_specs_small(),
        out_specs=pl.BlockSpec((CHUNK, D_CONV), lambda b: (b, 0)),
        out_shape=jax.ShapeDtypeStruct((dec_batch * CHUNK, D_CONV), BF16),
        scratch_shapes=[pltpu.VMEM((CONV_HALO + CHUNK, D_CONV), F32)],
        compiler_params=_params(("arbitrary",)),
        name="conv_sample",
    )(halo, u, dw_w, dw_b, ln_g, ln_b)


ROUTER_ROWS = 32


def _first_max(vals):
    m = vals[0]
    for v in vals[1:]:
        m = jnp.maximum(m, v)
    hots, taken = [], None
    for v in vals:
        hit = v == m
        if taken is None:
            hots.append(hit)
            taken = hit
        else:
            hots.append(jnp.logical_and(hit, jnp.logical_not(taken)))
            taken = jnp.logical_or(taken, hit)
    return m, hots


def _route(lt):
    gl = [lt[g:g + 1, :] for g in range(N_GROUPS)]
    gmax, gsel = _first_max(gl)
    denom = jnp.exp(gl[0] - gmax)
    for g in range(1, N_GROUPS):
        denom = denom + jnp.exp(gl[g] - gmax)
    p_group = 1.0 / denom
    el = []
    for e in range(EXPERTS_PER_GROUP):
        v = jnp.zeros_like(gmax)
        for g in range(N_GROUPS):
            row = N_GROUPS + g * EXPERTS_PER_GROUP + e
            v = jnp.where(gsel[g], lt[row:row + 1, :], v)
        el.append(v)
    m1, hot1 = _first_max(el)
    ninf = jnp.full_like(m1, -jnp.inf)
    m2, hot2 = _first_max([jnp.where(hot1[e], ninf, el[e]) for e in range(EXPERTS_PER_GROUP)])
    hot2 = [jnp.logical_and(hot2[e], jnp.logical_not(hot1[e])) for e in range(EXPERTS_PER_GROUP)]
    t = jnp.exp(m2 - m1)
    w1 = p_group / (1.0 + t)
    w2 = p_group * t / (1.0 + t)
    izero = jnp.zeros(gmax.shape, I32)
    gid = izero
    for g in range(1, N_GROUPS):
        gid = jnp.where(gsel[g], g, gid)
    i1, i2 = izero, izero
    for e in range(1, EXPERTS_PER_GROUP):
        i1 = jnp.where(hot1[e], e, i1)
        i2 = jnp.where(hot2[e], e, i2)
    return gid * EXPERTS_PER_GROUP + i1, gid * EXPERTS_PER_GROUP + i2, w1, w2


def _merge_kernel(xp_ref, xs_ref, ap_ref, as_ref, cp_ref, cs_ref, ga_ref, gc_ref, wao_ref, wco_ref, wo_ref,
                  g1_ref, b1_ref, wr_ref, br_ref,
                  h_ref, ri_ref, rw_ref, cnt_ref, xg_hbm,
                  base_ref, dstv_ref, dsts_ref, cntv_ref, cnts_ref, zrow_ref, sem,
                  *, alpha, prompt_blocks, cap):
    i = pl.program_id(0)
    n_steps = pl.num_programs(0)
    tm = h_ref.shape[0]

    @pl.when(i == 0)
    def _():
        base_ref[...] = jnp.zeros_like(base_ref)
        dstv_ref[...] = jnp.zeros_like(dstv_ref)
        zrow_ref[...] = jnp.zeros_like(zrow_ref)

    def row_copy(r, dst_row):
        return pltpu.make_async_copy(h_ref.at[pl.ds(r, 1)], xg_hbm.at[pl.ds(dst_row, 1)], sem.at[0])

    def body(x_ref, a_ref, c_ref):
        a_br = jnp.dot(a_ref[...], wao_ref[...], preferred_element_type=F32)
        c_br = jnp.dot(c_ref[...], wco_ref[...], preferred_element_type=F32)
        mix = ga_ref[...].astype(F32) * a_br + gc_ref[...].astype(F32) * c_br
        m = jnp.dot(mix.astype(BF16), wo_ref[...], preferred_element_type=F32)
        h = _layer_norm(alpha * x_ref[...] + m, g1_ref[...], b1_ref[...])
        h_ref[...] = h

        lt = lax.dot_general(wr_ref[...], h, (((1,), (1,)), ((), ())), precision=lax.Precision.HIGHEST,
                             preferred_element_type=F32) + br_ref[...]
        ea, eb, wa, wb = _route(lt)

        eid = lax.broadcasted_iota(I32, (N_EXPERTS, tm), 0)
        hot_a = eid == ea
        hot_b = eid == eb
        hot = jnp.logical_or(hot_a, hot_b).astype(F32)
        earlier = (lax.broadcasted_iota(I32, (tm, tm), 0) < lax.broadcasted_iota(I32, (tm, tm), 1)).astype(BF16)
        seen = base_ref[:, 0:1] + jnp.dot(hot.astype(BF16), earlier, preferred_element_type=F32)
        rank_a = jnp.sum(jnp.where(hot_a, seen, 0.0), axis=0, keepdims=True).astype(I32)
        rank_b = jnp.sum(jnp.where(hot_b, seen, 0.0), axis=0, keepdims=True).astype(I32)
        base_ref[...] = base_ref[...] + jnp.sum(hot, axis=1, keepdims=True)

        ri_ref[...] = jnp.zeros_like(ri_ref)
        rw_ref[...] = jnp.zeros_like(rw_ref)
        ri_ref[0:1, :] = ea
        ri_ref[1:2, :] = eb
        ri_ref[2:3, :] = rank_a
        ri_ref[3:4, :] = rank_b
        rw_ref[0:1, :] = wa
        rw_ref[1:2, :] = wb
        cnt_ref[...] = base_ref[...].astype(I32)

        dstv_ref[0:1, :] = ea * cap + rank_a
        dstv_ref[1:2, :] = eb * cap + rank_b
        to_smem = pltpu.make_async_copy(dstv_ref, dsts_ref, sem.at[1])
        to_smem.start()
        to_smem.wait()

        def issue(r, carry):
            for k in range(2):
                row_copy(r, dsts_ref[k, r]).start()
            return carry

        def drain(r, carry):
            for k in range(2):
                row_copy(r, 0).wait()
            return carry

        lax.fori_loop(0, tm, issue, 0, unroll=4)
        lax.fori_loop(0, tm, drain, 0, unroll=4)

    @pl.when(i < prompt_blocks)
    def _():
        body(xp_ref, ap_ref, cp_ref)

    @pl.when(i >= prompt_blocks)
    def _():
        body(xs_ref, as_ref, cs_ref)

    @pl.when(i == n_steps - 1)
    def _():
        cntv_ref[...] = base_ref[...].astype(I32)
        to_smem = pltpu.make_async_copy(cntv_ref, cnts_ref, sem.at[1])
        to_smem.start()
        to_smem.wait()
        for e in range(N_EXPERTS):
            count = cnts_ref[e, 0]
            n_pad = lax.rem(MOE_TILE - lax.rem(count, MOE_TILE), MOE_TILE)
            row0 = e * cap + count

            def zcopy(r):
                return pltpu.make_async_copy(zrow_ref.at[pl.ds(0, 1)], xg_hbm.at[pl.ds(row0 + r, 1)], sem.at[0])

            def zissue(r, carry):
                zcopy(r).start()
                return carry

            def zdrain(r, carry):
                zcopy(r).wait()
                return carry

            lax.fori_loop(0, n_pad, zissue, 0)
            lax.fori_loop(0, n_pad, zdrain, 0)


def _merge(x_p, x_s, attn_p, attn_s, conv_p, conv_s, gates, w_ao, w_co, w_o, ln_g, ln_b, w_router_t, b_router,
           alpha, cap):
    n_p, n_s = x_p.shape[0], x_s.shape[0]
    n = n_p + n_s
    tm = _pick_tile((n_p, n_s), (256, 128))
    pb = n_p // tm
    first, second = _two_part(pb)
    full = lambda i: (0, 0)
    once = pl.Buffered(1)
    blk_p = lambda d: pl.BlockSpec((tm, d), lambda i: (first(i), 0))
    blk_s = lambda d: pl.BlockSpec((tm, d), lambda i: (second(i), 0), pipeline_mode=once)
    return pl.pallas_call(
        functools.partial(_merge_kernel, alpha=alpha, prompt_blocks=pb, cap=cap),
        grid=(n // tm,),
        in_specs=[
            blk_p(D_MODEL), blk_s(D_MODEL), blk_p(D_ATTN), blk_s(D_ATTN), blk_p(D_CONV), blk_s(D_CONV),
            pl.BlockSpec((tm, D_MODEL), lambda i: (i, 0)),
            pl.BlockSpec((tm, D_MODEL), lambda i: (i, 1)),
            pl.BlockSpec((D_ATTN, D_MODEL), full, pipeline_mode=once),
            pl.BlockSpec((D_CONV, D_MODEL), full, pipeline_mode=once),
            pl.BlockSpec((D_MODEL, D_MODEL), full, pipeline_mode=once),
            pl.BlockSpec((1, D_MODEL), full),
            pl.BlockSpec((1, D_MODEL), full),
            pl.BlockSpec((ROUTER_ROWS, D_MODEL), full),
            pl.BlockSpec((ROUTER_ROWS, 1), full),
        ],
        out_specs=[
            pl.BlockSpec((tm, D_MODEL), lambda i: (i, 0)),
            pl.BlockSpec((SUBLANES, tm), lambda i: (0, i)),
            pl.BlockSpec((SUBLANES, tm), lambda i: (0, i)),
            pl.BlockSpec((N_EXPERTS, LANES), full),
            pl.BlockSpec(memory_space=pl.ANY),
        ],
        out_shape=[
            jax.ShapeDtypeStruct((n, D_MODEL), F32),
            jax.ShapeDtypeStruct((SUBLANES, n), I32),
            jax.ShapeDtypeStruct((SUBLANES, n), F32),
            jax.ShapeDtypeStruct((N_EXPERTS, LANES), I32),
            jax.ShapeDtypeStruct((N_EXPERTS * cap, D_MODEL), F32),
        ],
        scratch_shapes=[
            pltpu.VMEM((N_EXPERTS, LANES), F32),
            pltpu.VMEM((SUBLANES, tm), I32),
            pltpu.SMEM((SUBLANES, tm), I32),
            pltpu.VMEM((N_EXPERTS, LANES), I32),
            pltpu.SMEM((N_EXPERTS, LANES), I32),
            pltpu.VMEM((SUBLANES, D_MODEL), F32),
            pltpu.SemaphoreType.DMA((2,)),
        ],
        compiler_params=_params(("arbitrary",)),
        name="merge_ln1_router",
    )(x_p, x_s, attn_p, attn_s, conv_p, conv_s, gates, gates, w_ao, w_co, w_o, ln_g, ln_b, w_router_t, b_router)


def _moe_kernel(te_ref, trow_ref, tv_ref, x_ref, wg_ref, wu_ref, wd_ref, y_ref):
    t = pl.program_id(0)

    @pl.when(tv_ref[t] == 1)
    def _():
        x = x_ref[...].astype(BF16)
        gate = jnp.dot(x, wg_ref[...], preferred_element_type=F32)
        up = jnp.dot(x, wu_ref[...], preferred_element_type=F32)
        hid = (gate * jax.nn.sigmoid(gate) * up).astype(BF16)
        y_ref[...] = jnp.dot(hid, wd_ref[...], preferred_element_type=F32)

    @pl.when(tv_ref[t] == 0)
    def _():
        y_ref[...] = jnp.zeros_like(y_ref)


def _moe(x_grouped, tile_expert, tile_row, tile_valid, w_gate, w_up, w_down):
    n_tiles = tile_expert.shape[0]
    grid_spec = pltpu.PrefetchScalarGridSpec(
        num_scalar_prefetch=3,
        grid=(n_tiles,),
        in_specs=[
            pl.BlockSpec((MOE_TILE, D_MODEL), lambda t, te, tr, tv: (tr[t], 0)),
            pl.BlockSpec((None, D_MODEL, D_EXPERT), lambda t, te, tr, tv: (te[t], 0, 0)),
            pl.BlockSpec((None, D_MODEL, D_EXPERT), lambda t, te, tr, tv: (te[t], 0, 0)),
            pl.BlockSpec((None, D_EXPERT, D_MODEL), lambda t, te, tr, tv: (te[t], 0, 0)),
        ],
        out_specs=pl.BlockSpec((MOE_TILE, D_MODEL), lambda t, te, tr, tv: (t, 0)),
    )
    return pl.pallas_call(
        _moe_kernel,
        grid_spec=grid_spec,
        out_shape=jax.ShapeDtypeStruct((n_tiles * MOE_TILE, D_MODEL), F32),
        compiler_params=_params(("arbitrary",)),
        name="routed_experts",
    )(tile_expert, tile_row, tile_valid, x_grouped, w_gate, w_up, w_down)


def _combine_kernel(pos_ref, nxt_ref, h_ref, w_ref, g2_ref, b2_ref, y_hbm, op_ref, os_ref, ybuf, sem,
                    *, alpha, rows, prompt_blocks):
    i = pl.program_id(0)
    n = pl.num_programs(0)
    slot = lax.rem(i, 2)

    def row_copy(row, sl, k, r):
        return pltpu.make_async_copy(y_hbm.at[pl.ds(row, 1)], ybuf.at[sl, k, pl.ds(r, 1)], sem.at[sl])

    def gather(idx_ref, sl):
        def body(r, carry):
            for k in range(2):
                row_copy(idx_ref[0, k, r], sl, k, r).start()
            return carry
        lax.fori_loop(0, rows, body, 0, unroll=4)

    def drain(sl):
        def body(r, carry):
            for k in range(2):
                row_copy(0, sl, k, r).wait()
            return carry
        lax.fori_loop(0, rows, body, 0, unroll=4)

    @pl.when(i == 0)
    def _():
        gather(pos_ref, 0)

    @pl.when(i + 1 < n)
    def _():
        gather(nxt_ref, 1 - slot)

    drain(slot)
    w = w_ref[...]
    moe = w[:, 0:1] * ybuf[slot, 0] + w[:, 1:2] * ybuf[slot, 1]
    out = _layer_norm(alpha * h_ref[...] + moe, g2_ref[...], b2_ref[...])

    @pl.when(i < prompt_blocks)
    def _():
        op_ref[...] = out

    @pl.when(i >= prompt_blocks)
    def _():
        os_ref[...] = out


def _combine(h, y_grouped, pos_blocks, w_cols, ln_g, ln_b, alpha, n_p):
    n = h.shape[0]
    n_blk, _, rows = pos_blocks.shape
    pb = n_p // rows
    first, second = _two_part(pb)
    full = lambda i: (0, 0)
    return pl.pallas_call(
        functools.partial(_combine_kernel, alpha=alpha, rows=rows, prompt_blocks=pb),
        grid=(n_blk,),
        in_specs=[
            pl.BlockSpec((1, 2, rows), lambda i: (i, 0, 0), memory_space=pltpu.SMEM),
            pl.BlockSpec((1, 2, rows), lambda i: (jnp.minimum(i + 1, n_blk - 1), 0, 0), memory_space=pltpu.SMEM),
            pl.BlockSpec((rows, D_MODEL), lambda i: (i, 0)),
            pl.BlockSpec((rows, 2), lambda i: (i, 0)),
            pl.BlockSpec((1, D_MODEL), full),
            pl.BlockSpec((1, D_MODEL), full),
            pl.BlockSpec(memory_space=pl.ANY),
        ],
        out_specs=[
            pl.BlockSpec((rows, D_MODEL), lambda i: (first(i), 0)),
            pl.BlockSpec((rows, D_MODEL), lambda i: (second(i), 0)),
        ],
        out_shape=[
            jax.ShapeDtypeStruct((n_p, D_MODEL), F32),
            jax.ShapeDtypeStruct((n - n_p, D_MODEL), F32),
        ],
        scratch_shapes=[pltpu.VMEM((2, 2, rows, D_MODEL), F32), pltpu.SemaphoreType.DMA((2,))],
        compiler_params=_params(("arbitrary",)),
        name="combine_ln2",
    )(pos_blocks, pos_blocks, h, w_cols, ln_g, ln_b, y_grouped)


def _tile_tables(count, route_idx, n_tok, cap, rows):
    size = (count + MOE_TILE - 1) // MOE_TILE * MOE_TILE
    end = jnp.cumsum(size)
    start = end - size
    n_tiles = -(-2 * n_tok // MOE_TILE) + N_EXPERTS
    tile_start = jnp.arange(n_tiles, dtype=I32) * MOE_TILE
    n_valid = end[-1] // MOE_TILE
    valid = tile_start < end[-1]
    expert = jnp.minimum(jnp.sum((tile_start[:, None] >= end[None, :]).astype(I32), axis=1), N_EXPERTS - 1)
    hot = expert[:, None] == jnp.arange(N_EXPERTS, dtype=I32)[None, :]
    tile_in_expert = (tile_start - jnp.sum(jnp.where(hot, start[None, :], 0), axis=1)) // MOE_TILE
    row_block = expert * (cap // MOE_TILE) + tile_in_expert
    last = jnp.maximum(n_valid - 1, 0)
    expert = jnp.where(valid, expert, expert[last])
    row_block = jnp.where(valid, row_block, row_block[last])
    ids, ranks = route_idx[0:2], route_idx[2:4]
    hot_pair = ids[:, :, None] == jnp.arange(N_EXPERTS, dtype=I32)[None, None, :]
    pos = ranks + jnp.sum(jnp.where(hot_pair, start[None, None, :], 0), axis=2)
    pos_blocks = pos.reshape(2, n_tok // rows, rows).transpose(1, 0, 2)
    return expert.astype(I32), row_block.astype(I32), valid.astype(I32), pos_blocks.astype(I32)


def _rel_bias_pairs(rel_table):
    n_far = CHUNK + LEFT_FRAMES - MAX_REL
    n_near = BAND - 1 - (LEFT_FRAMES - MAX_REL)
    far = jnp.broadcast_to(rel_table[:, 2 * MAX_REL:], (N_HEADS, n_far))
    near = rel_table[:, 2 * MAX_REL - n_near:2 * MAX_REL][:, ::-1]
    by_offset = jnp.concatenate([far, near], axis=1)
    rows = [by_offset[:, CHUNK - 1 - qi:CHUNK - 1 - qi + BAND] for qi in range(CHUNK)]
    return jnp.stack(rows, axis=1).reshape(HEAD_PAIRS, 2 * CHUNK, BAND)


def _encoder_layer(x_p, x_s, batch, seq, dec_batch, cache_k, cache_v, cache_conv, alpha,
                   w_in, b_in, rel_table, dw_w, dw_b, conv_ln_g, conv_ln_b, w_attn_out, w_conv_out, w_o,
                   ln1_g, ln1_b, w_rg, b_rg, w_re, b_re, w_gate, w_up, w_down, ln2_g, ln2_b):
    n_p, n_s = x_p.shape[0], x_s.shape[0]
    n_tok = n_p + n_s
    row = lambda v: v.reshape(1, -1).astype(F32)

    qkv, u, gates = _in_projection(x_p, x_s, _permute_in_columns(w_in).astype(BF16), row(_permute_in_columns(b_in)))

    bias2 = _rel_bias_pairs(rel_table.astype(F32))
    attn_p = _attention_prompt(qkv, bias2, batch, seq)
    attn_s, k_state_s, v_state_s = _attention_sample(
        qkv, cache_k.reshape(dec_batch, LEFT_FRAMES, D_ATTN), cache_v.reshape(dec_batch, LEFT_FRAMES, D_ATTN),
        bias2, n_p, dec_batch)

    conv_p = _conv_prompt(u, dw_w, row(dw_b), row(conv_ln_g), row(conv_ln_b), batch, seq)
    halo = jnp.pad(cache_conv, ((0, 0), (CONV_HALO - (CONV_WIDTH - 1), 0), (0, 0)))
    conv_s = _conv_sample(u, halo, dw_w, row(dw_b), row(conv_ln_g), row(conv_ln_b), n_p, dec_batch)

    n_logits = N_GROUPS + N_EXPERTS
    w_router_t = jnp.concatenate([w_rg, w_re.reshape(D_MODEL, N_EXPERTS)], axis=1).T
    w_router_t = jnp.pad(w_router_t, ((0, ROUTER_ROWS - n_logits), (0, 0)))
    b_router = jnp.pad(jnp.concatenate([b_rg, b_re.reshape(-1)]), (0, ROUTER_ROWS - n_logits))
    cap = -(-n_tok // MOE_TILE) * MOE_TILE
    h, route_idx, route_w, count, x_grouped = _merge(
        x_p, x_s, attn_p, attn_s, conv_p, conv_s, gates,
        w_attn_out.astype(BF16), w_conv_out.astype(BF16), w_o.astype(BF16), row(ln1_g), row(ln1_b),
        w_router_t.astype(F32), b_router.reshape(-1, 1).astype(F32), alpha, cap)

    rows = _pick_tile((n_p, n_s), (256, 128, 64))
    tile_expert, tile_row, tile_valid, pos_blocks = _tile_tables(count[:, 0], route_idx, n_tok, cap, rows)
    y_grouped = _moe(x_grouped, tile_expert, tile_row, tile_valid, w_gate.astype(BF16), w_up.astype(BF16),
                     w_down.astype(BF16))
    y_p, y_s = _combine(h, y_grouped, pos_blocks, route_w[0:2].T, row(ln2_g), row(ln2_b), alpha, n_p)

    keep = min(LEFT_FRAMES, seq)
    tail = qkv[:n_p].reshape(batch, seq, 3, N_HEADS, HEAD_DIM)[:, seq - keep:]
    conv_state_p = u[:n_p].reshape(batch, seq, D_CONV)[:, seq - (CONV_WIDTH - 1):]
    conv_state_s = u[n_p:].reshape(dec_batch, CHUNK, D_CONV)[:, CHUNK - (CONV_WIDTH - 1):]
    states = (tail[:, :, 1], tail[:, :, 2], conv_state_p,
              k_state_s.reshape(dec_batch, LEFT_FRAMES, N_HEADS, HEAD_DIM),
              v_state_s.reshape(dec_batch, LEFT_FRAMES, N_HEADS, HEAD_DIM), conv_state_s)
    return y_p, y_s, states


def kernel(x_prompt, x_sample, cache_attn_k, cache_attn_v, cache_conv, w_in, b_in, rel_table, dw_w, dw_b,
           conv_ln_g, conv_ln_b, w_attn_out, w_conv_out, w_o, ln1_g, ln1_b, w_router_group, b_router_group,
           w_router_expert, b_router_expert, w_gate, w_up, w_down, ln2_g, ln2_b):
    depth = w_in.shape[0]
    batch, seq, _ = x_prompt.shape
    dec_batch, dec_seq, _ = x_sample.shape
    assert dec_seq == CHUNK and cache_attn_k.shape[2] == LEFT_FRAMES and seq % LEFT_FRAMES == 0
    assert cache_conv.shape[2] == CONV_WIDTH - 1
    alpha = (2.0 * depth) ** 0.25
    y_p = x_prompt.reshape(batch * seq, D_MODEL)
    y_s = x_sample.reshape(dec_batch * dec_seq, D_MODEL)
    collected = [[] for _ in range(6)]
    for l in range(depth):
        y_p, y_s, states = _encoder_layer(
            y_p, y_s, batch, seq, dec_batch, cache_attn_k[l], cache_attn_v[l], cache_conv[l], alpha,
            w_in[l], b_in[l], rel_table[l], dw_w[l], dw_b[l], conv_ln_g[l], conv_ln_b[l], w_attn_out[l],
            w_conv_out[l], w_o[l], ln1_g[l], ln1_b[l], w_router_group[l], b_router_group[l], w_router_expert[l],
            b_router_expert[l], w_gate[l], w_up[l], w_down[l], ln2_g[l], ln2_b[l])
        for acc, s in zip(collected, states):
            acc.append(s)
    return (y_p.reshape(batch, seq, D_MODEL), y_s.reshape(dec_batch, dec_seq, D_MODEL)) + tuple(
        jnp.stack(c) for c in collected)
```
